```python
import math
import jax, jax.numpy as jnp
from jax import lax
import numpy as np


D_MODEL = 1024
BATCH = 16
SEQ = 2048
DEPTH = 1
DEC_BATCH = 32
DEC_SEQ = 2048
PAST_LEN = 128

N_META = 16
CONV_CH = D_MODEL // 2
CONV_K = 31
MLA_HEADS = 8
QK_NOPE = 64
QK_ROPE = 32
V_DIM = 64
Q_LORA = D_MODEL // 4
KV_LORA = D_MODEL // 4
D_FF = ((8 * D_MODEL // 3 + 255) // 256) * 256
Q_BLOCK = 128
ROPE_THETA = 10000.0
EPS = 1e-6
IN_WIDTH = 2 * CONV_CH + Q_LORA + KV_LORA + QK_ROPE
MIX_WIDTH = CONV_CH + MLA_HEADS * V_DIM
ATTN_SCALE = 1.0 / math.sqrt(QK_NOPE + QK_ROPE)

kernel_name = 'hybrid_conformer_mla_encoder'


def _rms(x, g):
    x32 = x.astype(jnp.float32)
    y = x32 * lax.rsqrt(jnp.mean(x32 * x32, axis=-1, keepdims=True) + EPS)
    return (y * g.astype(jnp.float32)).astype(x.dtype)


def _layernorm(x, g, b):
    x32 = x.astype(jnp.float32)
    mu = jnp.mean(x32, axis=-1, keepdims=True)
    var = jnp.mean(jnp.square(x32 - mu), axis=-1, keepdims=True)
    y = (x32 - mu) * lax.rsqrt(var + EPS)
    return (y * g.astype(jnp.float32) + b.astype(jnp.float32)).astype(x.dtype)


def _rope_tables(length, dtype):
    inv = 1.0 / (ROPE_THETA ** (jnp.arange(0, QK_ROPE, 2, dtype=jnp.float32) / QK_ROPE))
    ang = jnp.arange(length, dtype=jnp.float32)[:, None] * inv[None, :]
    return jnp.cos(ang).astype(dtype), jnp.sin(ang).astype(dtype)


def _rope(x, cos, sin):
    x1, x2 = jnp.split(x, 2, axis=-1)
    return jnp.concatenate([x1 * cos - x2 * sin, x2 * cos + x1 * sin], axis=-1)


def _attend_block(qn, qr, kn, kr, v):
    s = jnp.einsum('bqhd,bkhd->bhqk', qn, kn) + jnp.einsum('bqhd,bkd->bhqk', qr, kr)
    p = jax.nn.softmax(s.astype(jnp.float32) * ATTN_SCALE, axis=-1).astype(v.dtype)
    return jnp.einsum('bhqk,bkhd->bqhd', p, v)


def _mla_attention(qn, qr, kn, kr, v):
    b, l, h, _ = qn.shape
    s = l - N_META
    nb = s // Q_BLOCK
    out_meta = _attend_block(qn[:, :N_META], qr[:, :N_META], kn, kr, v)
    qn_r = qn[:, N_META:].reshape(b, nb, Q_BLOCK, h, QK_NOPE).transpose(1, 0, 2, 3, 4)
    qr_r = qr[:, N_META:].reshape(b, nb, Q_BLOCK, h, QK_ROPE).transpose(1, 0, 2, 3, 4)
    out_r = lax.map(lambda qs: _attend_block(qs[0], qs[1], kn, kr, v), (qn_r, qr_r))
    out_r = out_r.transpose(1, 0, 2, 3, 4).reshape(b, s, h, V_DIM)
    return jnp.concatenate([out_meta, out_r], axis=1)


def _depthwise_conv(u, w, bias):
    c = u.shape[-1]
    y = lax.conv_general_dilated(
        u, w[:, None, :].astype(u.dtype), window_strides=(1,),
        padding=[((CONV_K - 1) // 2, (CONV_K - 1) // 2)],
        dimension_numbers=('NWC', 'WIO', 'NWC'), feature_group_count=c)
    return y + bias


def _encode(x, meta_tokens, attn_norm_g, w_in, q_norm_g, w_q_up, kv_norm_g, w_kv_up,
            conv_dw_w, conv_dw_b, conv_ln_g, conv_ln_b, w_out, ffn_norm_g,
            w_gate, w_up, w_down, final_norm_g):
    b = x.shape[0]
    meta = jnp.broadcast_to(meta_tokens[None].astype(x.dtype), (b, N_META, D_MODEL))
    h = jnp.concatenate([meta, x], axis=1)
    l = h.shape[1]
    cos, sin = _rope_tables(l, h.dtype)
    for i in range(DEPTH):
        hn = _rms(h, attn_norm_g[i])
        z = hn @ w_in[i]
        c_val, c_gate, cq, ckv, k_rope = jnp.split(
            z, [CONV_CH, 2 * CONV_CH, 2 * CONV_CH + Q_LORA, 2 * CONV_CH + Q_LORA + KV_LORA], axis=-1)
        u = c_val * jax.nn.sigmoid(c_gate)
        u = _depthwise_conv(u, conv_dw_w[i], conv_dw_b[i])
        u = jax.nn.silu(_layernorm(u, conv_ln_g[i], conv_ln_b[i]))
        q = (_rms(cq, q_norm_g[i]) @ w_q_up[i]).reshape(b, l, MLA_HEADS, QK_NOPE + QK_ROPE)
        q_nope, q_rope = jnp.split(q, [QK_NOPE], axis=-1)
        kv = (_rms(ckv, kv_norm_g[i]) @ w_kv_up[i]).reshape(b, l, MLA_HEADS, QK_NOPE + V_DIM)
        k_nope, v = jnp.split(kv, [QK_NOPE], axis=-1)
        q_rope = _rope(q_rope, cos[None, :, None, :], sin[None, :, None, :])
        k_rope = _rope(k_rope, cos[None], sin[None])
        o = _mla_attention(q_nope, q_rope, k_nope, k_rope, v).reshape(b, l, MLA_HEADS * V_DIM)
        h = h + jnp.concatenate([u, o], axis=-1) @ w_out[i]
        hf = _rms(h, ffn_norm_g[i])
        h = h + (jax.nn.silu(hf @ w_gate[i]) * (hf @ w_up[i])) @ w_down[i]
    return _rms(h, final_norm_g)[:, N_META:]


def setup_inputs(seed: int = 0) -> dict:
    key = jax.random.key(seed)
    ks = jax.random.split(key, 20)
    f32 = jnp.float32

    def nrm(k, shape, scale):
        return jax.random.normal(k, shape, f32) * scale

    def gain(k, shape):
        return 1.0 + 0.02 * jax.random.normal(k, shape, f32)

    return {
        'x_prompt': nrm(ks[0], (BATCH, SEQ, D_MODEL), 1.0),
        'x_sample': nrm(ks[1], (DEC_BATCH, DEC_SEQ, D_MODEL), 1.0),
        'meta_tokens': nrm(ks[2], (N_META, D_MODEL), 1.0),
        'attn_norm_g': gain(ks[3], (DEPTH, D_MODEL)),
        'w_in': nrm(ks[4], (DEPTH, D_MODEL, IN_WIDTH), D_MODEL ** -0.5),
        'q_norm_g': gain(ks[5], (DEPTH, Q_LORA)),
        'w_q_up': nrm(ks[6], (DEPTH, Q_LORA, MLA_HEADS * (QK_NOPE + QK_ROPE)), Q_LORA ** -0.5),
        'kv_norm_g': gain(ks[7], (DEPTH, KV_LORA)),
        'w_kv_up': nrm(ks[8], (DEPTH, KV_LORA, MLA_HEADS * (QK_NOPE + V_DIM)), KV_LORA ** -0.5),
        'conv_dw_w': nrm(ks[9], (DEPTH, CONV_K, CONV_CH), CONV_K ** -0.5),
        'conv_dw_b': nrm(ks[10], (DEPTH, CONV_CH), 0.02),
        'conv_ln_g': gain(ks[11], (DEPTH, CONV_CH)),
        'conv_ln_b': nrm(ks[12], (DEPTH, CONV_CH), 0.02),
        'w_out': nrm(ks[13], (DEPTH, MIX_WIDTH, D_MODEL), MIX_WIDTH ** -0.5),
        'ffn_norm_g': gain(ks[14], (DEPTH, D_MODEL)),
        'w_gate': nrm(ks[15], (DEPTH, D_MODEL, D_FF), D_MODEL ** -0.5),
        'w_up': nrm(ks[16], (DEPTH, D_MODEL, D_FF), D_MODEL ** -0.5),
        'w_down': nrm(ks[17], (DEPTH, D_FF, D_MODEL), D_FF ** -0.5),
        'final_norm_g': gain(ks[18], (D_MODEL,)),
    }


def reference(x_prompt, x_sample, meta_tokens, attn_norm_g, w_in, q_norm_g, w_q_up, kv_norm_g,
              w_kv_up, conv_dw_w, conv_dw_b, conv_ln_g, conv_ln_b, w_out, ffn_norm_g,
              w_gate, w_up, w_down, final_norm_g):
    y_prompt = _encode(x_prompt, meta_tokens, attn_norm_g, w_in, q_norm_g, w_q_up, kv_norm_g,
                       w_kv_up, conv_dw_w, conv_dw_b, conv_ln_g, conv_ln_b, w_out, ffn_norm_g,
                       w_gate, w_up, w_down, final_norm_g)
    y_sample = _encode(x_sample, meta_tokens, attn_norm_g, w_in, q_norm_g, w_q_up, kv_norm_g,
                       w_kv_up, conv_dw_w, conv_dw_b, conv_ln_g, conv_ln_b, w_out, ffn_norm_g,
                       w_gate, w_up, w_down, final_norm_g)
    return (y_prompt, y_sample)
```

```python
import functools
import math

import jax
import jax.numpy as jnp
from jax import lax
from jax.experimental import pallas as pl
from jax.experimental.pallas import tpu as pltpu

D_MODEL = 1024
N_META = 16
CONV_CH = 512
CONV_K = 31
CONV_PAD = (CONV_K - 1) // 2
HEADS = 8
QK_NOPE = 64
QK_ROPE = 32
V_DIM = 64
Q_LORA = 256
KV_LORA = 256
D_FF = 2816
ROPE_THETA = 10000.0
EPS = 1e-6
ATTN_SCALE = 1.0 / math.sqrt(QK_NOPE + QK_ROPE)

LANES = 128
HEAD_W = LANES
IN_CONV = 2 * CONV_CH
IN_Q = IN_CONV + Q_LORA
IN_KV = IN_Q + KV_LORA
IN_WIDTH_P = IN_KV + 2 * LANES
VMEM_LIMIT = 56 * 1024 * 1024

BF16 = jnp.bfloat16
F32 = jnp.float32


def _dot(a, b):
    return jnp.dot(a, b, preferred_element_type=F32)


def _dot_t(a, b):
    return lax.dot_general(a, b, (((1,), (1,)), ((), ())), preferred_element_type=F32)


def _rms_rows(x, g):
    ms = jnp.mean(x * x, axis=-1, keepdims=True)
    return x * lax.rsqrt(ms + EPS) * g


def _proj_kernel(x_ref, g_ref, win_ref, qg_ref, wq_ref, kvg_ref, wkv_ref,
                 tq_ref, ck_ref, sk_ref, u_ref, q_ref, k_ref, v_ref):
    x = x_ref[...]
    hn = _rms_rows(x, g_ref[...]).astype(BF16)

    c_val = _dot(hn, win_ref[:, 0:CONV_CH])
    c_gate = _dot(hn, win_ref[:, CONV_CH:IN_CONV])
    u_ref[...] = (c_val * jax.nn.sigmoid(c_gate)).astype(u_ref.dtype)

    cq = _dot(hn, win_ref[:, IN_CONV:IN_Q])
    cqn = _rms_rows(cq, qg_ref[...]).astype(BF16)
    q_raw = _dot(cqn, wq_ref[...])
    tq = tq_ref[...]
    for h in range(HEADS):
        q_ref[h] = (q_raw[:, h * HEAD_W:(h + 1) * HEAD_W] * tq).astype(q_ref.dtype)

    ckv = _dot(hn, win_ref[:, IN_Q:IN_KV])
    ckvn = _rms_rows(ckv, kvg_ref[...]).astype(BF16)
    kv = _dot(ckvn, wkv_ref[...])
    kr = _dot(hn, win_ref[:, IN_KV:IN_KV + LANES])
    krot = _dot(hn, win_ref[:, IN_KV + LANES:IN_WIDTH_P])
    kp = kr * ck_ref[...] + krot * sk_ref[...]
    low = lax.broadcasted_iota(jnp.int32, kp.shape, 1) < QK_NOPE
    for h in range(HEADS):
        kvh = kv[:, h * HEAD_W:(h + 1) * HEAD_W]
        k_ref[h] = jnp.where(low, kvh, kp).astype(k_ref.dtype)
        v_ref[h] = jnp.where(low, 1.0, kvh).astype(v_ref.dtype)


def _proj(x, prm, tabs, tm):
    b, s, _ = x.shape
    tq, ck, sk = tabs
    full = lambda shape: pl.BlockSpec(shape, lambda bi, i: (0,) * len(shape))
    tab = pl.BlockSpec((tm, LANES), lambda bi, i: (i, 0))
    head_out = pl.BlockSpec((None, HEADS, tm, HEAD_W), lambda bi, i: (bi, 0, i, 0))
    return pl.pallas_call(
        _proj_kernel,
        grid=(b, s // tm),
        in_specs=[
            pl.BlockSpec((None, tm, D_MODEL), lambda bi, i: (bi, i, 0)),
            full((1, D_MODEL)),
            full((D_MODEL, IN_WIDTH_P)),
            full((1, Q_LORA)),
            full((Q_LORA, HEADS * HEAD_W)),
            full((1, KV_LORA)),
            full((KV_LORA, HEADS * HEAD_W)),
            tab, tab, tab,
        ],
        out_specs=[
            pl.BlockSpec((None, tm, CONV_CH), lambda bi, i: (bi, i, 0)),
            head_out, head_out, head_out,
        ],
        out_shape=[
            jax.ShapeDtypeStruct((b, s, CONV_CH), BF16),
            jax.ShapeDtypeStruct((b, HEADS, s, HEAD_W), BF16),
            jax.ShapeDtypeStruct((b, HEADS, s, HEAD_W), BF16),
            jax.ShapeDtypeStruct((b, HEADS, s, HEAD_W), BF16),
        ],
        compiler_params=pltpu.CompilerParams(
            dimension_semantics=("parallel", "parallel"), vmem_limit_bytes=VMEM_LIMIT),
        name="proj",
    )(x, prm["attn_g"], prm["w_in"], prm["q_g"], prm["w_q"], prm["kv_g"], prm["w_kv"], tq, ck, sk)


CONV_HALO = 16
CONV_ROWS = 32


def _conv_kernel(u_ref, um_ref, w_ref, b_ref, g_ref, beta_ref, o_ref, win_ref, y_ref, *, tc):
    i = pl.program_id(1)
    n = pl.num_programs(1)
    t0 = pl.multiple_of(i * tc, tc)
    win_ref[CONV_HALO:CONV_HALO + tc, :] = u_ref[pl.ds(t0, tc), :].astype(F32)

    @pl.when(i == 0)
    def _():
        win_ref[0:CONV_HALO, :] = um_ref[...].astype(F32)

    @pl.when(i > 0)
    def _():
        win_ref[0:CONV_HALO, :] = u_ref[pl.ds(t0 - CONV_HALO, CONV_HALO), :].astype(F32)

    @pl.when(i == n - 1)
    def _():
        win_ref[CONV_HALO + tc:, :] = jnp.zeros((CONV_HALO, CONV_CH), F32)

    @pl.when(i < n - 1)
    def _():
        win_ref[CONV_HALO + tc:, :] = u_ref[pl.ds(t0 + tc, CONV_HALO), :].astype(F32)

    base = CONV_HALO - CONV_PAD
    for c in range(CONV_CH // LANES):
        cs = slice(c * LANES, (c + 1) * LANES)
        for r in range(tc // CONV_ROWS):
            r0 = r * CONV_ROWS
            acc = jnp.zeros((CONV_ROWS, LANES), F32) + b_ref[:, cs]
            for k in range(CONV_K):
                acc = acc + win_ref[base + r0 + k:base + r0 + k + CONV_ROWS, cs] * w_ref[k:k + 1, cs]
            y_ref[r0:r0 + CONV_ROWS, cs] = acc

    y = y_ref[...]
    mu = jnp.mean(y, axis=-1, keepdims=True)
    yc = y - mu
    var = jnp.mean(yc * yc, axis=-1, keepdims=True)
    z = yc * lax.rsqrt(var + EPS) * g_ref[...] + beta_ref[...]
    o_ref[...] = (z * jax.nn.sigmoid(z)).astype(o_ref.dtype)


def _conv(u, u_meta, prm, tc):
    b, s, _ = u.shape
    full = lambda shape: pl.BlockSpec(shape, lambda bi, i: (0,) * len(shape))
    return pl.pallas_call(
        functools.partial(_conv_kernel, tc=tc),
        grid=(b, s // tc),
        in_specs=[
            pl.BlockSpec((None, s, CONV_CH), lambda bi, i: (bi, 0, 0)),
            full((N_META, CONV_CH)),
            full((CONV_K, CONV_CH)),
            full((1, CONV_CH)), full((1, CONV_CH)), full((1, CONV_CH)),
        ],
        out_specs=pl.BlockSpec((None, tc, CONV_CH), lambda bi, i: (bi, i, 0)),
        out_shape=jax.ShapeDtypeStruct((b, s, CONV_CH), BF16),
        scratch_shapes=[
            pltpu.VMEM((tc + 2 * CONV_HALO, CONV_CH), F32),
            pltpu.VMEM((tc, CONV_CH), F32),
        ],
        compiler_params=pltpu.CompilerParams(
            dimension_semantics=("parallel", "arbitrary"), vmem_limit_bytes=VMEM_LIMIT),
        name="conv",
    )(u, u_meta, prm["conv_w"], prm["conv_b"], prm["ln_g"], prm["ln_b"])


def _attn_kernel(q_ref, k_ref, v_ref, km_ref, vm_ref, o_ref, *, kc):
    s_len = k_ref.shape[1]
    tq = q_ref.shape[1]
    low = lax.broadcasted_iota(jnp.int32, (tq, HEAD_W), 1) < V_DIM
    outs = []
    for h in range(HEADS):
        q = q_ref[h]
        s = _dot_t(q, km_ref[h])
        m = jnp.max(s, axis=-1, keepdims=True)
        p = jnp.exp(s - m)
        acc = _dot(p.astype(BF16), vm_ref[h])
        for c in range(s_len // kc):
            s = _dot_t(q, k_ref[h, c * kc:(c + 1) * kc, :])
            m_new = jnp.maximum(m, jnp.max(s, axis=-1, keepdims=True))
            alpha = jnp.exp(m - m_new)
            p = jnp.exp(s - m_new)
            acc = alpha * acc + _dot(p.astype(BF16), v_ref[h, c * kc:(c + 1) * kc, :])
            m = m_new
        outs.append(acc * (1.0 / acc[:, 0:1]))
    for j in range(HEADS // 2):
        even = pltpu.roll(outs[2 * j], V_DIM, 1)
        o_ref[:, j * LANES:(j + 1) * LANES] = jnp.where(low, even, outs[2 * j + 1]).astype(o_ref.dtype)


def _attn(q, k, v, k_meta, v_meta, tq, kc):
    b, _, s, _ = q.shape
    meta = pl.BlockSpec((HEADS, N_META, HEAD_W), lambda bi, i: (0, 0, 0))
    seq = pl.BlockSpec((None, HEADS, s, HEAD_W), lambda bi, i: (bi, 0, 0, 0))
    return pl.pallas_call(
        functools.partial(_attn_kernel, kc=kc),
        grid=(b, s // tq),
        in_specs=[
            pl.BlockSpec((None, HEADS, tq, HEAD_W), lambda bi, i: (bi, 0, i, 0)),
            seq, seq, meta, meta,
        ],
        out_specs=pl.BlockSpec((None, tq, HEADS * V_DIM), lambda bi, i: (bi, i, 0)),
        out_shape=jax.ShapeDtypeStruct((b, s, HEADS * V_DIM), BF16),
        compiler_params=pltpu.CompilerParams(
            dimension_semantics=("parallel", "arbitrary"), vmem_limit_bytes=VMEM_LIMIT),
        name="attn",
    )(q, k, v, k_meta, v_meta)


def _ffn_kernel(x_ref, uc_ref, o_ref, fg_ref, og_ref, wou_ref, woo_ref, wg_ref, wu_ref, wd_ref, y_ref):
    h1 = x_ref[...] + _dot(uc_ref[...], wou_ref[...]) + _dot(o_ref[...], woo_ref[...])
    hf = _rms_rows(h1, fg_ref[...]).astype(BF16)
    gate = _dot(hf, wg_ref[...])
    up = _dot(hf, wu_ref[...])
    a = (gate * jax.nn.sigmoid(gate) * up).astype(BF16)
    h2 = h1 + _dot(a, wd_ref[...])
    y_ref[...] = _rms_rows(h2, og_ref[...]).astype(y_ref.dtype)


def _ffn(x, uc, o, prm, tm):
    b, s, _ = x.shape
    full = lambda shape: pl.BlockSpec(shape, lambda bi, i: (0,) * len(shape),
                                      pipeline_mode=pl.Buffered(1))
    row = lambda w: pl.BlockSpec((None, tm, w), lambda bi, i: (bi, i, 0))
    return pl.pallas_call(
        _ffn_kernel,
        grid=(b, s // tm),
        in_specs=[
            row(D_MODEL), row(CONV_CH), row(HEADS * V_DIM), full((1, D_MODEL)), full((1, D_MODEL)),
            full((CONV_CH, D_MODEL)), full((HEADS * V_DIM, D_MODEL)),
            full((D_MODEL, D_FF)), full((D_MODEL, D_FF)), full((D_FF, D_MODEL)),
        ],
        out_specs=row(D_MODEL),
        out_shape=jax.ShapeDtypeStruct((b, s, D_MODEL), F32),
        compiler_params=pltpu.CompilerParams(
            dimension_semantics=("parallel", "parallel"), vmem_limit_bytes=VMEM_LIMIT),
        name="ffn",
    )(x, uc, o, prm["ffn_g"], prm["final_g"], prm["w_out_u"], prm["w_out_o"], prm["w_gate"],
      prm["w_up"], prm["w_down"])


def _rot_cols(w):
    half = QK_ROPE // 2
    return jnp.concatenate([-w[:, half:], w[:, :half]], axis=1)


def _prepare(meta_tokens, attn_norm_g, w_in, q_norm_g, w_q_up, kv_norm_g, w_kv_up, conv_dw_w,
             conv_dw_b, conv_ln_g, conv_ln_b, w_out, ffn_norm_g, w_gate, w_up, w_down, final_norm_g):
    w = w_in[0]
    w_kr = w[:, IN_KV:]
    reps = LANES // QK_ROPE
    w_in_p = jnp.concatenate(
        [w[:, :IN_KV], jnp.tile(w_kr, (1, reps)), jnp.tile(_rot_cols(w_kr), (1, reps))], axis=1)

    wq = w_q_up[0].reshape(Q_LORA, HEADS, QK_NOPE + QK_ROPE)
    wq_n, wq_r = wq[..., :QK_NOPE], wq[..., QK_NOPE:]
    wq_rot = jnp.concatenate([-wq_r[..., QK_ROPE // 2:], wq_r[..., :QK_ROPE // 2]], axis=-1)
    wq_p = jnp.concatenate([wq_n, wq_r, wq_rot], axis=-1).reshape(Q_LORA, HEADS * HEAD_W)

    row = lambda v: v.reshape(1, -1).astype(F32)
    return {
        "attn_g": row(attn_norm_g[0]), "w_in": w_in_p.astype(BF16),
        "q_g": row(q_norm_g[0]), "w_q": wq_p.astype(BF16),
        "kv_g": row(kv_norm_g[0]), "w_kv": w_kv_up[0].astype(BF16),
        "conv_w": conv_dw_w[0].astype(F32), "conv_b": row(conv_dw_b[0]),
        "ln_g": row(conv_ln_g[0]), "ln_b": row(conv_ln_b[0]),
        "w_out_u": w_out[0, :CONV_CH].astype(BF16), "w_out_o": w_out[0, CONV_CH:].astype(BF16),
        "ffn_g": row(ffn_norm_g[0]),
        "w_gate": w_gate[0].astype(BF16), "w_up": w_up[0].astype(BF16), "w_down": w_down[0].astype(BF16),
        "final_g": row(final_norm_g),
    }


def _tables(length):
    inv = 1.0 / (ROPE_THETA ** (jnp.arange(0, QK_ROPE, 2, dtype=F32) / QK_ROPE))
    ang = jnp.arange(length, dtype=F32)[:, None] * inv[None, :]
    cos, sin = jnp.cos(ang), jnp.sin(ang)
    cos2 = jnp.concatenate([cos, cos], axis=1)
    sin2 = jnp.concatenate([sin, sin], axis=1)
    tq = ATTN_SCALE * jnp.concatenate([jnp.ones((length, QK_NOPE), F32), cos2, sin2], axis=1)
    reps = LANES // QK_ROPE
    return tq, jnp.tile(cos2, (1, reps)), jnp.tile(sin2, (1, reps))


def _encode(x, meta_parts, prm, tabs, cfg):
    u_meta, k_meta, v_meta = meta_parts
    u, q, k, v = _proj(x, prm, tabs, cfg["tm_proj"])
    uc = _conv(u, u_meta, prm, cfg["tc"])
    o = _attn(q, k, v, k_meta, v_meta, cfg["tq"], cfg["kc"])
    return _ffn(x, uc, o, prm, cfg["tm_ffn"])


CONFIG = {"tm_proj": 512, "tc": 256, "tq": 256, "kc": 512, "tm_ffn": 512}


def _forward(xs, params, cfg):
    prm = _prepare(*params)
    seq = xs[0].shape[1]
    tabs = _tables(N_META + seq)
    meta_tabs = tuple(t[:N_META] for t in tabs)
    seq_tabs = tuple(t[N_META:] for t in tabs)
    meta = params[0].astype(F32)[None]
    u_m, _, k_m, v_m = _proj(meta, prm, meta_tabs, N_META)
    meta_parts = (u_m[0], k_m[0], v_m[0])
    return tuple(_encode(x, meta_parts, prm, seq_tabs, cfg) for x in xs)


def kernel(x_prompt, x_sample, meta_tokens, attn_norm_g, w_in, q_norm_g, w_q_up, kv_norm_g, w_kv_up,
           conv_dw_w, conv_dw_b, conv_ln_g, conv_ln_b, w_out, ffn_norm_g, w_gate, w_up, w_down,
           final_norm_g):
    params = (meta_tokens, attn_norm_g, w_in, q_norm_g, w_q_up, kv_norm_g, w_kv_up, conv_dw_w,
              conv_dw_b, conv_ln_g, conv_ln_b, w_out, ffn_norm_g, w_gate, w_up, w_down, final_norm_g)
    return _forward((x_prompt, x_sample), params, CONFIG)
```

```python
import functools
import math

import jax
import jax.numpy as jnp
from jax import lax
from jax.experimental import pallas as pl
from jax.experimental.pallas import tpu as pltpu

D_MODEL = 1024
N_META = 16
CONV_CH = 512
CONV_K = 31
CONV_PAD = (CONV_K - 1) // 2
HEADS = 8
QK_NOPE = 64
QK_ROPE = 32
V_DIM = 64
Q_LORA = 256
KV_LORA = 256
D_FF = 2816
ROPE_THETA = 10000.0
EPS = 1e-6
ATTN_SCALE = 1.0 / math.sqrt(QK_NOPE + QK_ROPE)
LOG2E = math.log2(math.e)

LANES = 128
SUBLANES = 8
HEAD_W = LANES
IN_CONV = 2 * CONV_CH
IN_Q = IN_CONV + Q_LORA
IN_KV = IN_Q + KV_LORA
IN_WIDTH_P = IN_KV + 2 * LANES
VMEM_LIMIT = 56 * 1024 * 1024

BF16 = jnp.bfloat16
F32 = jnp.float32


def _dot(a, b):
    return jnp.dot(a, b, preferred_element_type=F32)


def _dot_t(a, b):
    return lax.dot_general(a, b, (((1,), (1,)), ((), ())), preferred_element_type=F32)


def _rms_rows(x, g):
    ms = jnp.mean(x * x, axis=-1, keepdims=True)
    return x * lax.rsqrt(ms + EPS) * g


def _proj_kernel(x_ref, g_ref, win_ref, qg_ref, wqt_ref, kvg_ref, wk_ref, wvt_ref,
                 tqe_ref, tqo_ref, ck_ref, sk_ref, u_ref, q_ref, k_ref, v_ref):
    x = x_ref[...]
    hn = _rms_rows(x, g_ref[...]).astype(BF16)

    c_val = _dot(hn, win_ref[:, 0:CONV_CH])
    c_gate = _dot(hn, win_ref[:, CONV_CH:IN_CONV])
    u_ref[...] = (c_val * jax.nn.sigmoid(c_gate)).astype(u_ref.dtype)

    cq = _dot(hn, win_ref[:, IN_CONV:IN_Q])
    cqn = _rms_rows(cq, qg_ref[...]).astype(BF16)
    q_t = _dot_t(wqt_ref[...], cqn)
    tabs = (tqe_ref[...], tqo_ref[...])
    for h in range(HEADS):
        q_ref[h] = (q_t[h * HEAD_W:(h + 1) * HEAD_W, :] * tabs[h % 2]).astype(q_ref.dtype)

    ckv = _dot(hn, win_ref[:, IN_Q:IN_KV])
    ckvn = _rms_rows(ckv, kvg_ref[...]).astype(BF16)
    v_t = _dot_t(wvt_ref[...], ckvn)
    for h in range(HEADS):
        v_ref[h] = v_t[h * V_DIM:(h + 1) * V_DIM, :].astype(v_ref.dtype)

    kn = _dot(ckvn, wk_ref[...])
    kr = _dot(hn, win_ref[:, IN_KV:IN_KV + LANES])
    krot = _dot(hn, win_ref[:, IN_KV + LANES:IN_WIDTH_P])
    kp = kr * ck_ref[...] + krot * sk_ref[...]
    low = lax.broadcasted_iota(jnp.int32, kp.shape, 1) < QK_NOPE
    for j in range(HEADS // 2):
        pair = kn[:, j * LANES:(j + 1) * LANES]
        k_ref[2 * j] = jnp.where(low, pair, kp).astype(k_ref.dtype)
        k_ref[2 * j + 1] = jnp.where(low, kp, pair).astype(k_ref.dtype)


def _proj(x, prm, tabs, tm):
    b, s, _ = x.shape
    tqe, tqo, ck, sk = tabs
    full = lambda shape: pl.BlockSpec(shape, lambda bi, i: (0,) * len(shape))
    tab = pl.BlockSpec((tm, LANES), lambda bi, i: (i, 0))
    tab_t = pl.BlockSpec((HEAD_W, tm), lambda bi, i: (0, i))
    return pl.pallas_call(
        _proj_kernel,
        grid=(b, s // tm),
        in_specs=[
            pl.BlockSpec((None, tm, D_MODEL), lambda bi, i: (bi, i, 0)),
            full((1, D_MODEL)),
            full((D_MODEL, IN_WIDTH_P)),
            full((1, Q_LORA)),
            full((HEADS * HEAD_W, Q_LORA)),
            full((1, KV_LORA)),
            full((KV_LORA, HEADS * QK_NOPE)),
            full((HEADS * V_DIM, KV_LORA)),
            tab_t, tab_t, tab, tab,
        ],
        out_specs=[
            pl.BlockSpec((None, tm, CONV_CH), lambda bi, i: (bi, i, 0)),
            pl.BlockSpec((None, HEADS, HEAD_W, tm), lambda bi, i: (bi, 0, 0, i)),
            pl.BlockSpec((None, HEADS, tm, HEAD_W), lambda bi, i: (bi, 0, i, 0)),
            pl.BlockSpec((None, HEADS, V_DIM, tm), lambda bi, i: (bi, 0, 0, i)),
        ],
        out_shape=[
            jax.ShapeDtypeStruct((b, s, CONV_CH), BF16),
            jax.ShapeDtypeStruct((b, HEADS, HEAD_W, s), BF16),
            jax.ShapeDtypeStruct((b, HEADS, s, HEAD_W), BF16),
            jax.ShapeDtypeStruct((b, HEADS, V_DIM, s), BF16),
        ],
        compiler_params=pltpu.CompilerParams(
            dimension_semantics=("parallel", "parallel"), vmem_limit_bytes=VMEM_LIMIT),
        name="proj",
    )(x, prm["attn_g"], prm["w_in"], prm["q_g"], prm["w_qt"], prm["kv_g"], prm["w_k"], prm["w_vt"],
      tqe, tqo, ck, sk)


CONV_HALO = 16
CONV_ROWS = 64
CONV_BASE = CONV_HALO - CONV_PAD


def _conv_rows(win_ref, w_ref, bias, r0, cs):
    n_out = CONV_ROWS // SUBLANES
    n_in = n_out + (CONV_BASE + CONV_K - 1 + SUBLANES - 1) // SUBLANES
    tiles = [win_ref[r0 + SUBLANES * j:r0 + SUBLANES * (j + 1), cs] for j in range(n_in)]
    sub = lax.broadcasted_iota(jnp.int32, (SUBLANES, LANES), 0)
    acc = [bias] * n_out
    for r in range(SUBLANES):
        taps = [k for k in range(CONV_K) if (CONV_BASE + k) % SUBLANES == r]
        if not taps:
            continue
        if r == 0:
            shifted = tiles
        else:
            rolled = [pltpu.roll(t, SUBLANES - r, 0) for t in tiles]
            keep = sub < SUBLANES - r
            shifted = [jnp.where(keep, rolled[j], rolled[j + 1]) for j in range(n_in - 1)]
        for k in taps:
            a = (CONV_BASE + k) // SUBLANES
            wk = w_ref[k:k + 1, cs]
            for i in range(n_out):
                acc[i] = acc[i] + shifted[a + i] * wk
    return acc


def _conv_kernel(u_ref, um_ref, w_ref, b_ref, g_ref, beta_ref, o_ref, win_ref, y_ref, *, tc):
    i = pl.program_id(1)
    n = pl.num_programs(1)
    t0 = pl.multiple_of(i * tc, tc)
    win_ref[CONV_HALO:CONV_HALO + tc, :] = u_ref[pl.ds(t0, tc), :].astype(F32)

    @pl.when(i == 0)
    def _():
        win_ref[0:CONV_HALO, :] = um_ref[...].astype(F32)

    @pl.when(i > 0)
    def _():
        win_ref[0:CONV_HALO, :] = u_ref[pl.ds(t0 - CONV_HALO, CONV_HALO), :].astype(F32)

    @pl.when(i == n - 1)
    def _():
        win_ref[CONV_HALO + tc:, :] = jnp.zeros((CONV_HALO, CONV_CH), F32)

    @pl.when(i < n - 1)
    def _():
        win_ref[CONV_HALO + tc:, :] = u_ref[pl.ds(t0 + tc, CONV_HALO), :].astype(F32)

    for c in range(CONV_CH // LANES):
        cs = slice(c * LANES, (c + 1) * LANES)
        bias = jnp.broadcast_to(b_ref[:, cs], (SUBLANES, LANES))
        for r in range(tc // CONV_ROWS):
            acc = _conv_rows(win_ref, w_ref, bias, r * CONV_ROWS, cs)
            for j, a in enumerate(acc):
                y_ref[r * CONV_ROWS + SUBLANES * j:r * CONV_ROWS + SUBLANES * (j + 1), cs] = a

    y = y_ref[...]
    mu = jnp.mean(y, axis=-1, keepdims=True)
    yc = y - mu
    var = jnp.mean(yc * yc, axis=-1, keepdims=True)
    z = yc * lax.rsqrt(var + EPS) * g_ref[...] + beta_ref[...]
    o_ref[...] = (z * jax.nn.sigmoid(z)).astype(o_ref.dtype)


def _conv(u, u_meta, prm, tc):
    b, s, _ = u.shape
    full = lambda shape: pl.BlockSpec(shape, lambda bi, i: (0,) * len(shape))
    return pl.pallas_call(
        functools.partial(_conv_kernel, tc=tc),
        grid=(b, s // tc),
        in_specs=[
            pl.BlockSpec((None, s, CONV_CH), lambda bi, i: (bi, 0, 0)),
            full((N_META, CONV_CH)),
            full((CONV_K, CONV_CH)),
            full((1, CONV_CH)), full((1, CONV_CH)), full((1, CONV_CH)),
        ],
        out_specs=pl.BlockSpec((None, tc, CONV_CH), lambda bi, i: (bi, i, 0)),
        out_shape=jax.ShapeDtypeStruct((b, s, CONV_CH), BF16),
        scratch_shapes=[
            pltpu.VMEM((tc + 2 * CONV_HALO, CONV_CH), F32),
            pltpu.VMEM((tc, CONV_CH), F32),
        ],
        compiler_params=pltpu.CompilerParams(
            dimension_semantics=("parallel", "arbitrary"), vmem_limit_bytes=VMEM_LIMIT),
        name="conv",
    )(u, u_meta, prm["conv_w"], prm["conv_b"], prm["ln_g"], prm["ln_b"])


def _attn_kernel(q_ref, k_ref, v_ref, km_ref, vm_ref, o_ref, *score_refs):
    bufs = (score_refs[0:2], score_refs[2:4])

    def scores(h):
        s_ref, sm_ref = bufs[h % 2]
        q_t = q_ref[h]
        s_ref[...] = _dot(k_ref[h], q_t)
        sm_ref[...] = _dot(km_ref[h], q_t)

    def softmax_values(h):
        s_ref, sm_ref = bufs[h % 2]
        s = s_ref[...]
        sm = sm_ref[...]
        m = jnp.maximum(jnp.max(s, axis=0, keepdims=True), jnp.max(sm, axis=0, keepdims=True))
        p = jnp.exp2(s - m)
        pm = jnp.exp2(sm - m)
        den = jnp.sum(p, axis=0, keepdims=True) + jnp.sum(pm, axis=0, keepdims=True)
        num = _dot(v_ref[h], p.astype(BF16)) + _dot(vm_ref[h], pm.astype(BF16))
        return num * (1.0 / den)

    halves = []
    scores(0)
    for h in range(HEADS):
        if h + 1 < HEADS:
            scores(h + 1)
        halves.append(softmax_values(h))
    for j in range(HEADS // 2):
        pair_t = jnp.concatenate([halves[2 * j], halves[2 * j + 1]], axis=0)
        o_ref[:, j * LANES:(j + 1) * LANES] = pair_t.T.astype(o_ref.dtype)


def _attn(q_t, k, v_t, k_meta, v_meta_t, tq):
    b, _, s, _ = k.shape
    return pl.pallas_call(
        _attn_kernel,
        grid=(b, s // tq),
        in_specs=[
            pl.BlockSpec((None, HEADS, HEAD_W, tq), lambda bi, i: (bi, 0, 0, i)),
            pl.BlockSpec((None, HEADS, s, HEAD_W), lambda bi, i: (bi, 0, 0, 0)),
            pl.BlockSpec((None, HEADS, V_DIM, s), lambda bi, i: (bi, 0, 0, 0)),
            pl.BlockSpec((HEADS, N_META, HEAD_W), lambda bi, i: (0, 0, 0)),
            pl.BlockSpec((HEADS, V_DIM, N_META), lambda bi, i: (0, 0, 0)),
        ],
        out_specs=pl.BlockSpec((None, tq, HEADS * V_DIM), lambda bi, i: (bi, i, 0)),
        out_shape=jax.ShapeDtypeStruct((b, s, HEADS * V_DIM), BF16),
        scratch_shapes=[pltpu.VMEM((s, tq), F32), pltpu.VMEM((N_META, tq), F32)] * 2,
        compiler_params=pltpu.CompilerParams(
            dimension_semantics=("parallel", "arbitrary"), vmem_limit_bytes=VMEM_LIMIT),
        name="attn",
    )(q_t, k, v_t, k_meta, v_meta_t)


def _ffn_kernel(x_ref, uc_ref, o_ref, fg_ref, og_ref, wou_ref, woo_ref, wg_ref, wu_ref, wd_ref, y_ref):
    h1 = x_ref[...] + _dot(uc_ref[...], wou_ref[...]) + _dot(o_ref[...], woo_ref[...])
    hf = _rms_rows(h1, fg_ref[...]).astype(BF16)
    gate = _dot(hf, wg_ref[...])
    up = _dot(hf, wu_ref[...])
    a = (gate * jax.nn.sigmoid(gate) * up).astype(BF16)
    h2 = h1 + _dot(a, wd_ref[...])
    y_ref[...] = _rms_rows(h2, og_ref[...]).astype(y_ref.dtype)


def _ffn(x, uc, o, prm, tm):
    b, s, _ = x.shape
    full = lambda shape: pl.BlockSpec(shape, lambda bi, i: (0,) * len(shape),
                                      pipeline_mode=pl.Buffered(1))
    row = lambda w: pl.BlockSpec((None, tm, w), lambda bi, i: (bi, i, 0))
    return pl.pallas_call(
        _ffn_kernel,
        grid=(b, s // tm),
        in_specs=[
            row(D_MODEL), row(CONV_CH), row(HEADS * V_DIM), full((1, D_MODEL)), full((1, D_MODEL)),
            full((CONV_CH, D_MODEL)), full((HEADS * V_DIM, D_MODEL)),
            full((D_MODEL, D_FF)), full((D_MODEL, D_FF)), full((D_FF, D_MODEL)),
        ],
        out_specs=row(D_MODEL),
        out_shape=jax.ShapeDtypeStruct((b, s, D_MODEL), F32),
        compiler_params=pltpu.CompilerParams(
            dimension_semantics=("parallel", "parallel"), vmem_limit_bytes=VMEM_LIMIT),
        name="ffn",
    )(x, uc, o, prm["ffn_g"], prm["final_g"], prm["w_out_u"], prm["w_out_o"], prm["w_gate"],
      prm["w_up"], prm["w_down"])


def _rot_cols(w):
    half = QK_ROPE // 2
    return jnp.concatenate([-w[..., half:], w[..., :half]], axis=-1)


def _prepare(meta_tokens, attn_norm_g, w_in, q_norm_g, w_q_up, kv_norm_g, w_kv_up, conv_dw_w,
             conv_dw_b, conv_ln_g, conv_ln_b, w_out, ffn_norm_g, w_gate, w_up, w_down, final_norm_g):
    w = w_in[0]
    w_kr = w[:, IN_KV:]
    reps = LANES // QK_ROPE
    w_in_p = jnp.concatenate(
        [w[:, :IN_KV], jnp.tile(w_kr, (1, reps)), jnp.tile(_rot_cols(w_kr), (1, reps))], axis=1)

    wq = w_q_up[0].reshape(Q_LORA, HEADS // 2, 2, QK_NOPE + QK_ROPE)
    wq_n, wq_r = wq[..., :QK_NOPE], wq[..., QK_NOPE:]
    even = jnp.concatenate([wq_n[:, :, 0], wq_r[:, :, 0], _rot_cols(wq_r[:, :, 0])], axis=-1)
    odd = jnp.concatenate([wq_r[:, :, 1], _rot_cols(wq_r[:, :, 1]), wq_n[:, :, 1]], axis=-1)
    wq_p = jnp.stack([even, odd], axis=2).reshape(Q_LORA, HEADS * HEAD_W)

    wkv = w_kv_up[0].reshape(KV_LORA, HEADS, QK_NOPE + V_DIM)
    w_k = wkv[..., :QK_NOPE].reshape(KV_LORA, HEADS * QK_NOPE)
    w_v = wkv[..., QK_NOPE:].reshape(KV_LORA, HEADS * V_DIM)

    row = lambda v: v.reshape(1, -1).astype(F32)
    return {
        "attn_g": row(attn_norm_g[0]), "w_in": w_in_p.astype(BF16),
        "q_g": row(q_norm_g[0]), "w_qt": wq_p.T.astype(BF16),
        "kv_g": row(kv_norm_g[0]), "w_k": w_k.astype(BF16), "w_vt": w_v.T.astype(BF16),
        "conv_w": conv_dw_w[0].astype(F32), "conv_b": row(conv_dw_b[0]),
        "ln_g": row(conv_ln_g[0]), "ln_b": row(conv_ln_b[0]),
        "w_out_u": w_out[0, :CONV_CH].astype(BF16), "w_out_o": w_out[0, CONV_CH:].astype(BF16),
        "ffn_g": row(ffn_norm_g[0]),
        "w_gate": w_gate[0].astype(BF16), "w_up": w_up[0].astype(BF16), "w_down": w_down[0].astype(BF16),
        "final_g": row(final_norm_g),
    }


def _tables(length):
    inv = 1.0 / (ROPE_THETA ** (jnp.arange(0, QK_ROPE, 2, dtype=F32) / QK_ROPE))
    ang = jnp.arange(length, dtype=F32)[:, None] * inv[None, :]
    cos, sin = jnp.cos(ang), jnp.sin(ang)
    cos2 = jnp.concatenate([cos, cos], axis=1)
    sin2 = jnp.concatenate([sin, sin], axis=1)
    ones = jnp.ones((length, QK_NOPE), F32)
    scale = ATTN_SCALE * LOG2E
    tq_even = scale * jnp.concatenate([ones, cos2, sin2], axis=1)
    tq_odd = scale * jnp.concatenate([cos2, sin2, ones], axis=1)
    reps = LANES // QK_ROPE
    return tq_even.T, tq_odd.T, jnp.tile(cos2, (1, reps)), jnp.tile(sin2, (1, reps))


def _split_tables(tabs, lo, hi):
    tqe, tqo, ck, sk = tabs
    return tqe[:, lo:hi], tqo[:, lo:hi], ck[lo:hi], sk[lo:hi]


def _encode(x, meta_parts, prm, tabs, cfg):
    u_meta, k_meta, v_meta_t = meta_parts
    u, q_t, k, v_t = _proj(x, prm, tabs, cfg["tm_proj"])
    uc = _conv(u, u_meta, prm, cfg["tc"])
    o = _attn(q_t, k, v_t, k_meta, v_meta_t, cfg["tq"])
    return _ffn(x, uc, o, prm, cfg["tm_ffn"])


CONFIG = {"tm_proj": 512, "tc": 256, "tq": 256, "tm_ffn": 512}


def _forward(xs, params, cfg):
    prm = _prepare(*params)
    seq = xs[0].shape[1]
    tabs = _tables(N_META + seq)
    meta = params[0].astype(F32)[None]
    u_m, _, k_m, v_m = _proj(meta, prm, _split_tables(tabs, 0, N_META), N_META)
    meta_parts = (u_m[0], k_m[0], v_m[0])
    seq_tabs = _split_tables(tabs, N_META, N_META + seq)
    return tuple(_encode(x, meta_parts, prm, seq_tabs, cfg) for x in xs)


def kernel(x_prompt, x_sample, meta_tokens, attn_norm_g, w_in, q_norm_g, w_q_up, kv_norm_g, w_kv_up,
           conv_dw_w, conv_dw_b, conv_ln_g, conv_ln_b, w_out, ffn_norm_g, w_gate, w_up, w_down,
           final_norm_g):
    params = (meta_tokens, attn_norm_g, w_in, q_norm_g, w_q_up, kv_norm_g, w_kv_up, conv_dw_w,
              conv_dw_b, conv_ln_g, conv_ln_b, w_out, ffn_norm_g, w_gate, w_up, w_down, final_norm_g)
    return _forward((x_prompt, x_sample), params, CONFIG)
```
